```python
import math
import jax, jax.numpy as jnp
from jax import lax
import numpy as np

D_MODEL = 1024
BATCH = 8
SEQ = 4096
DEPTH = 2

GRID_W = 64
CTX_LEN = 256
CHUNK = 64
EPS = 1e-6
A_HEADS = 4
A_DK = 128
A_DV = 128
A_WIDTH = A_HEADS * A_DK
B_HEADS = 4
B_DK = 128
B_DV = 128
B_WIDTH = B_HEADS * B_DV
B_CONV = 5
C_WIDTH = 512
C_GROUP = 16
C_GROUPS = C_WIDTH // C_GROUP
C_STATE = 64
N_BRANCH = 3
D_FF = 4 * D_MODEL
IN_SIZES = (A_WIDTH, A_HEADS * A_DV, 2 * A_WIDTH, A_HEADS * A_DV,
            2 * B_HEADS * B_DK + B_HEADS * B_DV, B_WIDTH, 2 * B_HEADS, 2 * B_HEADS,
            C_WIDTH, N_BRANCH * D_MODEL)
IN_SPLITS = tuple(int(s) for s in np.cumsum(IN_SIZES)[:-1])
D_IN = int(sum(IN_SIZES))

kernel_name = 'hybrid_hgrn2_gdn_s5_prefix_dit'


def _rmsnorm(x, w):
    xf = x.astype(jnp.float32)
    y = xf * lax.rsqrt(jnp.mean(xf * xf, axis=-1, keepdims=True) + EPS)
    return (y * w.astype(jnp.float32)).astype(x.dtype)


def _modulate(h, shift, scale):
    return h * (1.0 + scale) + shift


def _heads(a, n_heads):
    b, n, _ = a.shape
    return a.reshape(b, n, n_heads, -1).transpose(0, 2, 1, 3)


def _merge_heads(a):
    b, h, n, d = a.shape
    return a.transpose(0, 2, 1, 3).reshape(b, n, h * d)


def _l2norm(a):
    return a * lax.rsqrt(jnp.sum(a * a, axis=-1, keepdims=True) + EPS)


def _to_chunks(a):
    b, h, n = a.shape[:3]
    return jnp.moveaxis(a.reshape((b, h, n // CHUNK, CHUNK) + a.shape[3:]), 2, 0)


def _from_chunks(a):
    a = jnp.moveaxis(a, 0, 2)
    return a.reshape(a.shape[:2] + (-1,) + a.shape[4:])


def _gla_chunked(q, k, v, log_f, s0):
    incl = jnp.tril(jnp.ones((CHUNK, CHUNK), dtype=bool))

    def step(s, inp):
        qc, kc, vc, gc = inp
        cum = jnp.cumsum(gc, axis=2)
        diff = cum[:, :, :, None, :] - cum[:, :, None, :, :]
        decay = jnp.where(incl[:, :, None], jnp.exp(jnp.minimum(diff, 0.0)), 0.0)
        att = jnp.einsum('bhtd,bhsd,bhtsd->bhts', qc, kc, decay)
        o = (jnp.einsum('bhts,bhsv->bhtv', att, vc)
             + jnp.einsum('bhtd,bhdv->bhtv', qc * jnp.exp(cum), s))
        last = cum[:, :, -1:, :]
        s = (s * jnp.exp(last[:, :, 0, :, None])
             + jnp.einsum('bhsd,bhsv->bhdv', kc * jnp.exp(last - cum), vc))
        return s, o

    s_fin, o = lax.scan(step, s0, (_to_chunks(q), _to_chunks(k), _to_chunks(v), _to_chunks(log_f)))
    return _from_chunks(o), s_fin


def _delta_chunked(q, k, v, beta, log_a, s0):
    incl = jnp.tril(jnp.ones((CHUNK, CHUNK), dtype=bool))
    strict = jnp.tril(jnp.ones((CHUNK, CHUNK), dtype=bool), k=-1)
    eye = jnp.eye(CHUNK, dtype=jnp.float32)
    dv = v.shape[-1]

    def step(s, inp):
        qc, kc, vc, bc, gc = inp
        cum = jnp.cumsum(gc, axis=-1)
        dmask = jnp.where(incl, jnp.exp(jnp.minimum(cum[..., :, None] - cum[..., None, :], 0.0)), 0.0)
        kb = kc * bc[..., None]
        m = jnp.where(strict, -jnp.einsum('bhtd,bhsd->bhts', kb, kc) * dmask, 0.0)
        rhs = jnp.concatenate([vc * bc[..., None], kb * jnp.exp(cum)[..., None]], axis=-1)
        sol = lax.linalg.triangular_solve(eye - m, rhs, left_side=True, lower=True, unit_diagonal=True)
        u, w = sol[..., :dv], sol[..., dv:]
        v_new = u - jnp.einsum('bhtd,bhdv->bhtv', w, s)
        att = jnp.einsum('bhtd,bhsd->bhts', qc, kc) * dmask
        o = (jnp.einsum('bhtd,bhdv->bhtv', qc * jnp.exp(cum)[..., None], s)
             + jnp.einsum('bhts,bhsv->bhtv', att, v_new))
        last = cum[..., -1:]
        s = (s * jnp.exp(last)[..., None]
             + jnp.einsum('bhsd,bhsv->bhdv', kc * jnp.exp(last - cum)[..., None], v_new))
        return s, o

    s_fin, o = lax.scan(step, s0, (_to_chunks(q), _to_chunks(k), _to_chunks(v),
                                   _to_chunks(beta), _to_chunks(log_a)))
    return _from_chunks(o), s_fin


def _bidir_scan(run, ctx_f, ctx_b, lat_f, lat_b, s0):
    flip = lambda arrs: [jnp.flip(a, axis=2) for a in arrs]
    yc_f, sc_f = run(*ctx_f, s0)
    yc_b, sc_b = run(*flip(ctx_b), s0)
    yl_f, _ = run(*lat_f, sc_f)
    yl_b, _ = run(*flip(lat_b), sc_b)
    return yc_f + jnp.flip(yc_b, axis=2), yl_f + jnp.flip(yl_b, axis=2)


def _hgrn2_prepare(q, i, f2, lb):
    b, n, _ = q.shape
    qh = _heads(jax.nn.silu(q.astype(jnp.float32)), A_HEADS)
    vh = _heads(i.astype(jnp.float32), A_HEADS)
    z = jnp.moveaxis(f2.astype(jnp.float32).reshape(b, n, 2, A_WIDTH), 2, 0)
    lbb = lb[:, None, None, :]
    fval = lbb + (1.0 - lbb) * jax.nn.sigmoid(z)
    args = [(qh, _heads(1.0 - fval[d], A_HEADS), vh, _heads(jnp.log(fval[d]), A_HEADS)) for d in range(2)]
    return args[0], args[1]


def _short_conv(x, w):
    ch = x.shape[-1]
    return lax.conv_general_dilated(x, w[:, None, :].astype(x.dtype), window_strides=(1,),
                                    padding=[(B_CONV // 2, B_CONV // 2)],
                                    dimension_numbers=('NWC', 'WIO', 'NWC'), feature_group_count=ch)


def _gdn_prepare(qkv, beta2, a2, conv_w, a_log, dt_bias):
    b, n, _ = qkv.shape
    f32 = jnp.float32
    qkv = jax.nn.silu(_short_conv(qkv, conv_w)).astype(f32)
    q, k, v = jnp.split(qkv, [B_HEADS * B_DK, 2 * B_HEADS * B_DK], axis=-1)
    q = _l2norm(_heads(q, B_HEADS)) * (B_DK ** -0.5)
    k = _l2norm(_heads(k, B_HEADS))
    v = _heads(v, B_HEADS)
    beta = jax.nn.sigmoid(beta2.astype(f32)).reshape(b, n, 2, B_HEADS).transpose(2, 0, 3, 1)
    a = a2.astype(f32).reshape(b, n, 2, B_HEADS).transpose(2, 0, 3, 1)
    g = -jnp.exp(a_log.astype(f32))[:, None, :, None] * jax.nn.softplus(a + dt_bias.astype(f32)[:, None, :, None])
    return (q, k, v, beta[0], g[0]), (q, k, v, beta[1], g[1])


def _s5_discretize(lam_re, lam_im, log_dt, b_re, b_im):
    f32 = jnp.float32
    lr = jnp.minimum(lam_re.astype(f32), -1e-4)
    li = lam_im.astype(f32)
    dt = jnp.exp(log_dt.astype(f32))[:, None]
    mag = jnp.exp(lr * dt)
    ar, ai = mag * jnp.cos(li * dt), mag * jnp.sin(li * dt)
    den = lr * lr + li * li
    cr = ((ar - 1.0) * lr + ai * li) / den
    ci = (ai * lr - (ar - 1.0) * li) / den
    br, bi = b_re.astype(f32), b_im.astype(f32)
    bbr = cr[..., None] * br - ci[..., None] * bi
    bbi = cr[..., None] * bi + ci[..., None] * br
    return ar, ai, bbr, bbi


def _complex_affine_combine(e1, e2):
    a1r, a1i, b1r, b1i = e1
    a2r, a2i, b2r, b2i = e2
    return (a2r * a1r - a2i * a1i, a2r * a1i + a2i * a1r,
            a2r * b1r - a2i * b1i + b2r, a2r * b1i + a2i * b1r + b2i)


def _s5_scan(u, ar, ai, bbr, bbi, x0r, x0i):
    n = u.shape[1]
    bur = jnp.einsum('gsc,bngc->bngs', bbr, u)
    bui = jnp.einsum('gsc,bngc->bngs', bbi, u)
    bur = bur.at[:, 0].add(ar * x0r - ai * x0i)
    bui = bui.at[:, 0].add(ar * x0i + ai * x0r)
    a_r = jnp.broadcast_to(ar, (1, n) + ar.shape)
    a_i = jnp.broadcast_to(ai, (1, n) + ai.shape)
    _, _, xr, xi = lax.associative_scan(_complex_affine_combine, (a_r, a_i, bur, bui), axis=1)
    return xr, xi


def _s5_readout(xr, xi, c_re, c_im):
    f32 = jnp.float32
    y = (jnp.einsum('gcs,bngs->bngc', c_re.astype(f32), xr)
         - jnp.einsum('gcs,bngs->bngc', c_im.astype(f32), xi))
    return y.reshape(y.shape[0], y.shape[1], C_WIDTH)


def _s5_mixer(u_ctx, u_lat, rows, lam_re, lam_im, log_dt, b_re, b_im, c_re, c_im, d_skip):
    f32 = jnp.float32
    dt = u_lat.dtype
    bsz, n, _ = u_lat.shape
    uc = u_ctx.astype(f32)
    ul = u_lat.astype(f32).reshape(bsz, rows, GRID_W, C_WIDTH).transpose(0, 2, 1, 3).reshape(bsz, n, C_WIDTH)
    gc = uc.reshape(bsz, uc.shape[1], C_GROUPS, C_GROUP)
    gl = ul.reshape(bsz, n, C_GROUPS, C_GROUP)
    x0 = jnp.zeros((bsz, C_GROUPS, C_STATE), f32)
    disc_f = _s5_discretize(lam_re[0], lam_im[0], log_dt[0], b_re, b_im)
    disc_b = _s5_discretize(lam_re[1], lam_im[1], log_dt[1], b_re, b_im)
    cfr, cfi = _s5_scan(gc, *disc_f, x0, x0)
    cbr, cbi = _s5_scan(jnp.flip(gc, axis=1), *disc_b, x0, x0)
    lfr, lfi = _s5_scan(gl, *disc_f, cfr[:, -1], cfi[:, -1])
    lbr, lbi = _s5_scan(jnp.flip(gl, axis=1), *disc_b, cbr[:, -1], cbi[:, -1])
    d = d_skip.astype(f32)
    yc = (_s5_readout(cfr, cfi, c_re, c_im) + jnp.flip(_s5_readout(cbr, cbi, c_re, c_im), axis=1) + d * uc)
    yl = (_s5_readout(lfr, lfi, c_re, c_im) + jnp.flip(_s5_readout(lbr, lbi, c_re, c_im), axis=1) + d * ul)
    yl = yl.reshape(bsz, GRID_W, rows, C_WIDTH).transpose(0, 2, 1, 3).reshape(bsz, n, C_WIDTH)
    return yc.astype(dt), yl.astype(dt)


def _norm_gate(o, w, gate):
    return _merge_heads(_rmsnorm(o, w)) * jax.nn.silu(gate.astype(jnp.float32))


def _branch_merge(oa, ga, ob, zb, yc, gate_pre, hgrn_norm_w, gdn_norm_w, w_glu, w_ba, w_bb, w_bc, w_o):
    dt = gate_pre.dtype
    ya = _norm_gate(oa, hgrn_norm_w, ga).astype(dt) @ w_ba
    yb = _norm_gate(ob, gdn_norm_w, zb).astype(dt) @ w_bb
    z = jax.nn.gelu(yc)
    ycc = (z * jax.nn.sigmoid(z @ w_glu)) @ w_bc
    g_a, g_b, g_c = jnp.split(jax.nn.sigmoid(gate_pre.astype(jnp.float32)).astype(dt), N_BRANCH, axis=-1)
    return (g_a * ya + g_b * yb + g_c * ycc) @ w_o


def _sqrelu_mlp(h, w1, w2):
    return jnp.square(jax.nn.relu(h @ w1)) @ w2


def setup_inputs(seed: int = 0) -> dict:
    key = jax.random.key(seed)
    ks = iter(jax.random.split(key, 40))
    f32 = jnp.float32
    nrm = lambda shape, scale: jax.random.normal(next(ks), shape, f32) * scale
    unif = lambda shape, lo, hi: jax.random.uniform(next(ks), shape, f32, minval=lo, maxval=hi)
    d = D_MODEL
    dt_init = jnp.exp(unif((DEPTH, 2, B_HEADS), math.log(1e-3), math.log(1e-1)))
    return {
        'x': nrm((BATCH, SEQ, d), 1.0),
        'c': nrm((BATCH, d), 1.0),
        'ctx': nrm((BATCH, CTX_LEN, d), 1.0),
        'c_ctx': nrm((d,), 1.0),
        'ada_w': nrm((DEPTH, d, 6 * d), 0.5 * d ** -0.5),
        'ada_b': nrm((DEPTH, 6 * d), 0.02),
        'norm1_w': 1.0 + nrm((DEPTH, d), 0.02),
        'w_in': nrm((DEPTH, d, D_IN), d ** -0.5),
        'hgrn_lb_logits': nrm((DEPTH, 2, A_WIDTH), 0.5),
        'hgrn_norm_w': 1.0 + nrm((DEPTH, A_DV), 0.02),
        'gdn_conv_w': nrm((DEPTH, B_CONV, 2 * B_HEADS * B_DK + B_HEADS * B_DV), B_CONV ** -0.5),
        'gdn_a_log': jnp.log(unif((DEPTH, 2, B_HEADS), 1.0, 16.0)),
        'gdn_dt_bias': dt_init + jnp.log(-jnp.expm1(-dt_init)),
        'gdn_norm_w': 1.0 + nrm((DEPTH, B_DV), 0.02),
        's5_lam_re': -0.5 + nrm((DEPTH, 2, C_GROUPS, C_STATE), 0.01),
        's5_lam_im': jnp.pi * jnp.arange(C_STATE, dtype=f32) + nrm((DEPTH, 2, C_GROUPS, C_STATE), 0.01),
        's5_log_dt': unif((DEPTH, 2, C_GROUPS), math.log(1e-3), math.log(1e-1)),
        's5_b_re': nrm((DEPTH, C_GROUPS, C_STATE, C_GROUP), (2 * C_GROUP) ** -0.5),
        's5_b_im': nrm((DEPTH, C_GROUPS, C_STATE, C_GROUP), (2 * C_GROUP) ** -0.5),
        's5_c_re': nrm((DEPTH, C_GROUPS, C_GROUP, C_STATE), C_STATE ** -0.5),
        's5_c_im': nrm((DEPTH, C_GROUPS, C_GROUP, C_STATE), C_STATE ** -0.5),
        's5_d': nrm((DEPTH, C_WIDTH), 1.0),
        's5_w_glu': nrm((DEPTH, C_WIDTH, C_WIDTH), C_WIDTH ** -0.5),
        'w_branch_a': nrm((DEPTH, A_HEADS * A_DV, d), (A_HEADS * A_DV) ** -0.5),
        'w_branch_b': nrm((DEPTH, B_WIDTH, d), B_WIDTH ** -0.5),
        'w_branch_c': nrm((DEPTH, C_WIDTH, d), C_WIDTH ** -0.5),
        'w_out': nrm((DEPTH, d, d), d ** -0.5),
        'norm2_w': 1.0 + nrm((DEPTH, d), 0.02),
        'w_ff1': nrm((DEPTH, d, D_FF), d ** -0.5),
        'w_ff2': nrm((DEPTH, D_FF, d), D_FF ** -0.5),
        'final_norm_w': 1.0 + nrm((d,), 0.02),
    }


def reference(x, c, ctx, c_ctx, ada_w, ada_b, norm1_w, w_in, hgrn_lb_logits, hgrn_norm_w,
              gdn_conv_w, gdn_a_log, gdn_dt_bias, gdn_norm_w,
              s5_lam_re, s5_lam_im, s5_log_dt, s5_b_re, s5_b_im, s5_c_re, s5_c_im, s5_d, s5_w_glu,
              w_branch_a, w_branch_b, w_branch_c, w_out, norm2_w, w_ff1, w_ff2, final_norm_w):
    bsz, n, _ = x.shape
    rows = n // GRID_W
    lb_all = jnp.cumsum(jax.nn.softmax(hgrn_lb_logits.astype(jnp.float32), axis=0), axis=0)
    lb_all = lb_all - lb_all[0]
    s0_a = jnp.zeros((bsz, A_HEADS, A_DK, A_DV), jnp.float32)
    s0_b = jnp.zeros((bsz, B_HEADS, B_DK, B_DV), jnp.float32)
    xl, xc = x, ctx
    for l in range(DEPTH):
        last = l == DEPTH - 1
        ml = [m[:, None, :] for m in jnp.split(jax.nn.silu(c) @ ada_w[l] + ada_b[l], 6, axis=-1)]
        mc = jnp.split(jax.nn.silu(c_ctx) @ ada_w[l] + ada_b[l], 6, axis=-1)
        pl = jnp.split(_modulate(_rmsnorm(xl, norm1_w[l]), ml[0], ml[1]) @ w_in[l], IN_SPLITS, axis=-1)
        pc = jnp.split(_modulate(_rmsnorm(xc, norm1_w[l]), mc[0], mc[1]) @ w_in[l], IN_SPLITS, axis=-1)
        a_cf, a_cb = _hgrn2_prepare(pc[0], pc[1], pc[2], lb_all[l])
        a_lf, a_lb = _hgrn2_prepare(pl[0], pl[1], pl[2], lb_all[l])
        oa_c, oa_l = _bidir_scan(_gla_chunked, a_cf, a_cb, a_lf, a_lb, s0_a)
        b_cf, b_cb = _gdn_prepare(pc[4], pc[6], pc[7], gdn_conv_w[l], gdn_a_log[l], gdn_dt_bias[l])
        b_lf, b_lb = _gdn_prepare(pl[4], pl[6], pl[7], gdn_conv_w[l], gdn_a_log[l], gdn_dt_bias[l])
        ob_c, ob_l = _bidir_scan(_delta_chunked, b_cf, b_cb, b_lf, b_lb, s0_b)
        yc_c, yc_l = _s5_mixer(pc[8], pl[8], rows, s5_lam_re[l], s5_lam_im[l], s5_log_dt[l],
                               s5_b_re[l], s5_b_im[l], s5_c_re[l], s5_c_im[l], s5_d[l])
        br = (hgrn_norm_w[l], gdn_norm_w[l], s5_w_glu[l], w_branch_a[l], w_branch_b[l], w_branch_c[l], w_out[l])
        xl = xl + ml[2] * _branch_merge(oa_l, pl[3], ob_l, pl[5], yc_l, pl[9], *br)
        xl = xl + ml[5] * _sqrelu_mlp(_modulate(_rmsnorm(xl, norm2_w[l]), ml[3], ml[4]), w_ff1[l], w_ff2[l])
        if not last:
            xc = xc + mc[2] * _branch_merge(oa_c, pc[3], ob_c, pc[5], yc_c, pc[9], *br)
            xc = xc + mc[5] * _sqrelu_mlp(_modulate(_rmsnorm(xc, norm2_w[l]), mc[3], mc[4]), w_ff1[l], w_ff2[l])
    return _rmsnorm(xl, final_norm_w)
```

```python
import functools

import numpy as np
import jax
import jax.numpy as jnp
from jax import lax
from jax.experimental import pallas as pl
from jax.experimental.pallas import tpu as pltpu

F32 = jnp.float32
BF16 = jnp.bfloat16
HI = lax.Precision.HIGHEST

EPS = 1e-6
GRID_W = 64
CHUNK = 64
HEADS = 4
HEAD_DIM = 128
WIDTH = HEADS * HEAD_DIM
S5_GROUP = 16
S5_GROUPS = WIDTH // S5_GROUP
S5_STATE = 64
CONV_TAPS = 5
TOK = 256
LANES = 128
SUBLANES = 8
VMEM_LIMIT = 56 * 1024 * 1024

COL_QKV = 0
COL_A = 3 * WIDTH
COL_GATE_B = COL_A + 5 * WIDTH
COL_CU = COL_GATE_B + WIDTH
COL_MERGE = COL_CU + WIDTH
N_LEVELS = 6


def _dot(a, b):
    return jnp.dot(a, b, preferred_element_type=F32)


def _dot_nt(a, b):
    return lax.dot_general(a, b, (((1,), (1,)), ((), ())), preferred_element_type=F32)


def _dot_tn(a, b):
    return lax.dot_general(a, b, (((0,), (0,)), ((), ())), preferred_element_type=F32)


def _dot_hi(a, b):
    return jnp.dot(a, b, preferred_element_type=F32, precision=HI)


def _dot_nt_hi(a, b):
    return lax.dot_general(a, b, (((1,), (1,)), ((), ())), preferred_element_type=F32, precision=HI)


def _dot_tn_hi(a, b):
    return lax.dot_general(a, b, (((0,), (0,)), ((), ())), preferred_element_type=F32, precision=HI)


def _bf(x):
    return x.astype(BF16)


def _sigmoid(x):
    return 1.0 / (1.0 + jnp.exp(-x))


def _silu(x):
    return x * _sigmoid(x)


def _params(*sem):
    return pltpu.CompilerParams(dimension_semantics=sem, vmem_limit_bytes=VMEM_LIMIT)


def _ada_kernel(c_ref, w_ref, b_ref, o_ref):
    o_ref[0] = _dot_hi(_silu(c_ref[...]), w_ref[0]) + b_ref[0]


def _ada(c_rows, ada_w, ada_b):
    depth, d, n6 = ada_w.shape
    r = c_rows.shape[0]
    tn = 1536
    return pl.pallas_call(
        _ada_kernel,
        grid=(depth, n6 // tn),
        in_specs=[pl.BlockSpec((r, d), lambda l, j: (0, 0)),
                  pl.BlockSpec((1, d, tn), lambda l, j: (l, 0, j)),
                  pl.BlockSpec((1, 1, tn), lambda l, j: (l, 0, j))],
        out_specs=pl.BlockSpec((1, r, tn), lambda l, j: (l, 0, j)),
        out_shape=jax.ShapeDtypeStruct((depth, r, n6), F32),
        compiler_params=_params("parallel", "parallel"),
        name="ada",
    )(c_rows, ada_w, ada_b.reshape(depth, 1, n6))


def _rms(x, w):
    return x * lax.rsqrt(jnp.mean(x * x, axis=-1, keepdims=True) + EPS) * w


def _inproj_kernel(x_ref, mod_ref, nw_ref, w_ref, o_ref, *, d):
    y = _rms(x_ref[0], nw_ref[...])
    h = y * (1.0 + mod_ref[0, :, d:2 * d]) + mod_ref[0, :, 0:d]
    o_ref[0] = _dot(_bf(h), w_ref[...])


def _inproj(xs, mod, nw, w, ncb):
    b, nt, d = xs.shape
    dn = w.shape[1]
    tn = dn // 5
    return pl.pallas_call(
        functools.partial(_inproj_kernel, d=d),
        grid=(dn // tn, b, nt // TOK),
        in_specs=[pl.BlockSpec((1, TOK, d), lambda n, i, j: (i, j, 0)),
                  pl.BlockSpec((1, 1, 6 * d), lambda n, i, j: (2 * i + (j >= ncb).astype(jnp.int32), 0, 0)),
                  pl.BlockSpec((1, d), lambda n, i, j: (0, 0)),
                  pl.BlockSpec((d, tn), lambda n, i, j: (0, n))],
        out_specs=pl.BlockSpec((1, TOK, tn), lambda n, i, j: (i, j, n)),
        out_shape=jax.ShapeDtypeStruct((b, nt, dn), F32),
        compiler_params=_params("parallel", "parallel", "parallel"),
        name="inproj",
    )(xs, mod, nw, w)


def _positions(d):
    t = np.arange(CHUNK)
    return t if d == 0 else CHUNK - 1 - t


def _hgrn_constants():
    rng = np.zeros((2, (N_LEVELS + 2) * CHUNK + SUBLANES, CHUNK), np.float32)
    msk = np.zeros((2, N_LEVELS + 1, CHUNK, CHUNK), np.float32)
    for d in range(2):
        p = _positions(d)
        pt, ps = p[:, None], p[None, :]
        for e in range(N_LEVELS):
            m = 2 ** (e + 1)
            mid = (pt // m) * m + m // 2 - 1
            qside = (pt % m) >= m // 2
            r = np.where(qside, (ps > mid) & (ps <= pt), (ps > pt) & (ps <= mid))
            rng[d, e * CHUNK:(e + 1) * CHUNK] = r
            msk[d, e] = ((pt // m) == (ps // m)) & qside & ((ps % m) < m // 2)
        rng[d, N_LEVELS * CHUNK:(N_LEVELS + 1) * CHUNK] = ps <= pt
        rng[d, (N_LEVELS + 1) * CHUNK:(N_LEVELS + 2) * CHUNK] = ps > pt
        rng[d, (N_LEVELS + 2) * CHUNK:] = 1.0
        msk[d, N_LEVELS] = pt == ps
    return jnp.asarray(rng, BF16), jnp.asarray(msk, F32)


def _gdn_constants():
    incl = np.zeros((2, CHUNK, CHUNK), np.float32)
    strict = np.zeros((2, CHUNK, CHUNK), np.float32)
    for d in range(2):
        p = _positions(d)
        incl[d] = p[None, :] <= p[:, None]
        strict[d] = p[None, :] < p[:, None]
    inclt = np.transpose(incl, (0, 2, 1))
    return (jnp.asarray(incl), jnp.asarray(inclt), jnp.asarray(strict),
            jnp.asarray(np.eye(CHUNK, dtype=np.float32)))


def _bwd_block(j, ncb, nblk):
    return jnp.where(j < ncb, ncb - 1 - j, ncb + nblk - 1 - j)


def _split3(g):
    g1 = _bf(g)
    r = g - g1.astype(F32)
    g2 = _bf(r)
    g3 = _bf(r - g2.astype(F32))
    return g1, g2, g3


def _hgrn_kernel(qf_ref, vf_ref, ff_ref, qb_ref, vb_ref, fb_ref, lbl_ref, rng_ref, msk_ref,
                 of_ref, ob_ref, st_ref, *, layer, nchunks):
    @pl.when(pl.program_id(1) == 0)
    def _():
        st_ref[...] = jnp.zeros_like(st_ref)

    logits = lbl_ref[...]
    ex = jnp.exp(logits - jnp.max(logits, axis=0, keepdims=True))
    sm = ex / jnp.sum(ex, axis=0, keepdims=True)
    lb_all = sm[0] * 0.0
    for i in range(1, layer + 1):
        lb_all = lb_all + sm[i]

    dirs = ((qf_ref, vf_ref, ff_ref, of_ref), (qb_ref, vb_ref, fb_ref, ob_ref))
    lv = N_LEVELS * CHUNK

    def body(c, carry):
        for d, (q_ref, v_ref, f_ref, o_ref) in enumerate(dirs):
            ci = c if d == 0 else nchunks - 1 - c
            rows = pl.ds(pl.multiple_of(ci * CHUNK, CHUNK), CHUNK)
            q = _silu(q_ref[0, rows, :])
            v = v_ref[0, rows, :]
            lb = lb_all[d:d + 1, :]
            fval = lb + (1.0 - lb) * _sigmoid(f_ref[0, rows, :])
            k = 1.0 - fval
            g1, g2, g3 = _split3(jnp.log(fval))
            r = rng_ref[d]
            x = jnp.exp(_dot(r, g1) + _dot(r, g2) + _dot(r, g3))
            for h in range(HEADS):
                sl = slice(h * HEAD_DIM, (h + 1) * HEAD_DIM)
                qh, kh, vh = q[:, sl], k[:, sl], v[:, sl]
                att = msk_ref[d, N_LEVELS] * _dot_nt(_bf(qh), _bf(kh))
                for e in range(N_LEVELS):
                    xe = x[e * CHUNK:(e + 1) * CHUNK, sl]
                    att = att + msk_ref[d, e] * _dot_nt(_bf(qh * xe), _bf(kh * xe))
                st = st_ref[d, h]
                o = _dot(_bf(att), _bf(vh)) + _dot_nt(_bf(qh * x[lv:lv + CHUNK, sl]), _bf(st))
                o_ref[0, rows, sl] = o
                kt = kh * x[lv + CHUNK:lv + 2 * CHUNK, sl]
                st_ref[d, h] = st * x[lv + 2 * CHUNK:lv + 2 * CHUNK + 1, sl] + _dot_tn(_bf(vh), _bf(kt))
        return carry

    lax.fori_loop(0, nchunks, body, 0)


def _hgrn(p, lb_logits, layer, ncb):
    b, nt, _ = p.shape
    nblk = nt // TOK
    rng, msk = _hgrn_constants()
    ca = COL_A // WIDTH

    def fwd(col):
        return pl.BlockSpec((1, TOK, WIDTH), lambda i, j: (i, j, col))

    def bwd(col):
        return pl.BlockSpec((1, TOK, WIDTH), lambda i, j: (i, _bwd_block(j, ncb, nblk), col))

    def full(a):
        nd = a.ndim
        return pl.BlockSpec(a.shape, lambda i, j: (0,) * nd)

    out = jax.ShapeDtypeStruct((b, nt, WIDTH), F32)
    return pl.pallas_call(
        functools.partial(_hgrn_kernel, layer=layer, nchunks=TOK // CHUNK),
        grid=(b, nblk),
        in_specs=[fwd(ca), fwd(ca + 1), fwd(ca + 2), bwd(ca), bwd(ca + 1), bwd(ca + 3),
                  full(lb_logits), full(rng), full(msk)],
        out_specs=[fwd(0), bwd(0)],
        out_shape=[out, out],
        scratch_shapes=[pltpu.VMEM((2, HEADS, HEAD_DIM, HEAD_DIM), F32)],
        compiler_params=_params("parallel", "arbitrary"),
        name="hgrn",
    )(p, p, p, p, p, p, lb_logits, rng, msk)


def _gdn_prep_kernel(cur_ref, prev_ref, next_ref, ba_ref, cw_ref, alog_ref, dtb_ref,
                     qkv_ref, bg_ref, pad_ref, *, ncb, nblk):
    j = pl.program_id(1)
    has_prev = jnp.logical_and(j != 0, j != ncb)
    has_next = jnp.logical_and(j != ncb - 1, j != nblk - 1)
    pad_ref[0:SUBLANES, :] = jnp.where(has_prev, prev_ref[0], 0.0)
    pad_ref[SUBLANES:SUBLANES + TOK, :] = cur_ref[0]
    pad_ref[SUBLANES + TOK:2 * SUBLANES + TOK, :] = jnp.where(has_next, next_ref[0], 0.0)
    half = CONV_TAPS // 2
    acc = pad_ref[pl.ds(SUBLANES - half, TOK), :] * cw_ref[0:1, :]
    for t in range(1, CONV_TAPS):
        acc = acc + pad_ref[pl.ds(SUBLANES - half + t, TOK), :] * cw_ref[t:t + 1, :]
    y = _silu(acc)
    for h in range(3 * HEADS):
        sl = slice(h * HEAD_DIM, (h + 1) * HEAD_DIM)
        a = y[:, sl]
        if h < 2 * HEADS:
            a = a * lax.rsqrt(jnp.sum(a * a, axis=-1, keepdims=True) + EPS)
        if h < HEADS:
            a = a * (HEAD_DIM ** -0.5)
        qkv_ref[0, :, sl] = a
    ba = ba_ref[0]
    sp_in = ba + dtb_ref[...]
    softplus = jnp.maximum(sp_in, 0.0) + jnp.log(1.0 + jnp.exp(-jnp.abs(sp_in)))
    col = lax.broadcasted_iota(jnp.int32, ba.shape, 1)
    bg_ref[0] = jnp.where(col < 2 * HEADS, _sigmoid(ba), -jnp.exp(alog_ref[...]) * softplus)


def _gdn_prep(p, conv_w, a_log, dt_bias, ncb):
    b, nt, dn = p.shape
    nblk = nt // TOK
    w3 = 3 * WIDTH
    rb = TOK // SUBLANES
    nrb = nt // SUBLANES
    lead = jnp.zeros((1, 2 * HEADS), F32)
    tail = jnp.zeros((1, LANES - 4 * HEADS), F32)
    alog = jnp.concatenate([lead, a_log.reshape(1, -1), tail], axis=1)
    dtb = jnp.concatenate([lead, dt_bias.reshape(1, -1), tail], axis=1)
    return pl.pallas_call(
        functools.partial(_gdn_prep_kernel, ncb=ncb, nblk=nblk),
        grid=(b, nblk),
        in_specs=[pl.BlockSpec((1, TOK, w3), lambda i, j: (i, j, 0)),
                  pl.BlockSpec((1, SUBLANES, w3), lambda i, j: (i, jnp.maximum(j * rb - 1, 0), 0)),
                  pl.BlockSpec((1, SUBLANES, w3), lambda i, j: (i, jnp.minimum((j + 1) * rb, nrb - 1), 0)),
                  pl.BlockSpec((1, TOK, LANES), lambda i, j: (i, j, dn // LANES - 1)),
                  pl.BlockSpec((CONV_TAPS, w3), lambda i, j: (0, 0)),
                  pl.BlockSpec((1, LANES), lambda i, j: (0, 0)),
                  pl.BlockSpec((1, LANES), lambda i, j: (0, 0))],
        out_specs=[pl.BlockSpec((1, TOK, w3), lambda i, j: (i, j, 0)),
                   pl.BlockSpec((1, TOK, LANES), lambda i, j: (i, j, 0))],
        out_shape=[jax.ShapeDtypeStruct((b, nt, w3), F32), jax.ShapeDtypeStruct((b, nt, LANES), F32)],
        scratch_shapes=[pltpu.VMEM((TOK + 2 * SUBLANES, w3), F32)],
        compiler_params=_params("parallel", "parallel"),
        name="gdn_prep",
    )(p, p, p, p, conv_w, alog, dtb)


def _gdn_kernel(qf_ref, kf_ref, vf_ref, gf_ref, qb_ref, kb_ref, vb_ref, gb_ref,
                incl_ref, inclt_ref, strict_ref, eye_ref, of_ref, ob_ref, st_ref, *, nchunks):
    @pl.when(pl.program_id(1) == 0)
    def _():
        st_ref[...] = jnp.zeros_like(st_ref)

    dirs = ((qf_ref, kf_ref, vf_ref, gf_ref, of_ref), (qb_ref, kb_ref, vb_ref, gb_ref, ob_ref))
    eye = eye_ref[...]

    def body(c, carry):
        for d, (q_ref, k_ref, v_ref, g_ref, o_ref) in enumerate(dirs):
            ci = c if d == 0 else nchunks - 1 - c
            rows = pl.ds(pl.multiple_of(ci * CHUNK, CHUNK), CHUNK)
            q = q_ref[0, rows, :]
            k = k_ref[0, rows, :]
            v = v_ref[0, rows, :]
            bg = g_ref[0, rows, :]
            incl, inclt, strict = incl_ref[d], inclt_ref[d], strict_ref[d]
            for h in range(HEADS):
                sl = slice(h * HEAD_DIM, (h + 1) * HEAD_DIM)
                qh, kh, vh = q[:, sl], k[:, sl], v[:, sl]
                cb = d * HEADS + h
                beta = bg[:, cb:cb + 1]
                gcol = bg[:, 2 * HEADS + cb:2 * HEADS + cb + 1]
                grow = jnp.sum(eye * gcol, axis=0, keepdims=True)
                cum_col = jnp.sum(incl * grow, axis=1, keepdims=True)
                cum_row = jnp.sum(inclt * gcol, axis=0, keepdims=True)
                last = jnp.sum(gcol, axis=0, keepdims=True)
                dm = incl * jnp.exp(jnp.minimum(cum_col - cum_row, 0.0))
                kbeta = kh * beta
                a = -(strict * _dot_nt_hi(kbeta, kh) * dm)
                t = eye + a
                pw = a
                for _ in range(N_LEVELS - 1):
                    pw = _dot_hi(pw, pw)
                    t = t + _dot_hi(t, pw)
                ecum = jnp.exp(cum_col)
                u = _dot_hi(t, vh * beta)
                w = _dot_hi(t, kbeta * ecum)
                st = st_ref[d, h]
                stb = _bf(st)
                v_new = u - _dot_nt(_bf(w), stb)
                att = _dot_nt(_bf(qh), _bf(kh)) * dm
                o_ref[0, rows, sl] = _dot_nt(_bf(qh * ecum), stb) + _dot(_bf(att), _bf(v_new))
                kt = kh * jnp.exp(last - cum_col)
                st_ref[d, h] = st * jnp.exp(last) + _dot_tn(_bf(v_new), _bf(kt))
        return carry

    lax.fori_loop(0, nchunks, body, 0)


def _gdn(qkv, bg, ncb):
    b, nt, _ = qkv.shape
    nblk = nt // TOK
    consts = _gdn_constants()

    def fwd(col, w=WIDTH):
        return pl.BlockSpec((1, TOK, w), lambda i, j: (i, j, col))

    def bwd(col, w=WIDTH):
        return pl.BlockSpec((1, TOK, w), lambda i, j: (i, _bwd_block(j, ncb, nblk), col))

    def full(a):
        nd = a.ndim
        return pl.BlockSpec(a.shape, lambda i, j: (0,) * nd)

    out = jax.ShapeDtypeStruct((b, nt, WIDTH), F32)
    return pl.pallas_call(
        functools.partial(_gdn_kernel, nchunks=TOK // CHUNK),
        grid=(b, nblk),
        in_specs=[fwd(0), fwd(1), fwd(2), fwd(0, LANES), bwd(0), bwd(1), bwd(2), bwd(0, LANES)]
                 + [full(a) for a in consts],
        out_specs=[fwd(0), bwd(0)],
        out_shape=[out, out],
        scratch_shapes=[pltpu.VMEM((2, HEADS, HEAD_DIM, HEAD_DIM), F32)],
        compiler_params=_params("parallel", "arbitrary"),
        name="gdn",
    )(qkv, qkv, qkv, bg, qkv, qkv, qkv, bg, *consts)


def _cis_pow(p, lrdt, ang):
    m = jnp.exp(p * lrdt)
    return m * jnp.cos(p * ang), m * jnp.sin(p * ang)


def _s5_param_kernel(lre_ref, lim_ref, ldt_ref, lrer_ref, limr_ref, ldtr_ref, bre_ref, bim_ref,
                     cre_ref, cim_ref, d_ref, tile_ref, m_ref, pt_ref, q_ref, hop_ref, *, t):
    w = S5_GROUP * t
    tile = tile_ref[...]
    lane2 = lax.broadcasted_iota(jnp.int32, (1, 2 * w), 1)
    lag = lane2 // S5_GROUP - (t - 1)
    lane1 = lax.broadcasted_iota(jnp.int32, (1, w), 1)
    pos = lane1 // S5_GROUP
    cre_t = _dot_hi(cre_ref[0], tile)
    cim_t = _dot_hi(cim_ref[0], tile)
    z = jnp.zeros((S5_GROUP, 2 * w), F32)
    for d in range(2):
        lr = jnp.minimum(lre_ref[0, d], -1e-4)
        li = lim_ref[0, d]
        dt = jnp.exp(ldt_ref[0, d])
        lrdt, ang = lr * dt, li * dt
        mag = jnp.exp(lrdt)
        ar, ai = mag * jnp.cos(ang), mag * jnp.sin(ang)
        den = lr * lr + li * li
        cr = ((ar - 1.0) * lr + ai * li) / den
        ci = (ai * lr - (ar - 1.0) * li) / den
        bre, bim = bre_ref[0], bim_ref[0]
        bbr = cr * bre - ci * bim
        bbi = cr * bim + ci * bre
        sgn = 1 if d == 0 else -1
        valid = (sgn * lag >= 0) & (lag <= t - 1)
        pr, pi = _cis_pow(jnp.maximum(sgn * lag, 0).astype(F32), lrdt, ang)
        clr = jnp.where(valid, cre_t * pr - cim_t * pi, 0.0)
        cli = jnp.where(valid, cre_t * pi + cim_t * pr, 0.0)
        z = z + _dot_tn_hi(bbr, clr) - _dot_tn_hi(bbi, cli)
        steps = (t - 1 - pos) if d == 0 else pos
        pr, pi = _cis_pow(steps.astype(F32), lrdt, ang)
        btr = _dot_hi(bbr, tile[:, :w])
        bti = _dot_hi(bbi, tile[:, :w])
        pt_ref[0, d, 0:S5_STATE, :] = btr * pr - bti * pi
        pt_ref[0, d, S5_STATE:2 * S5_STATE, :] = btr * pi + bti * pr
        steps = (pos + 1) if d == 0 else (t - pos)
        pr, pi = _cis_pow(steps.astype(F32), lrdt, ang)
        q_ref[0, d, 0:S5_STATE, :] = cre_t[:, :w] * pr - cim_t[:, :w] * pi
        q_ref[0, d, S5_STATE:2 * S5_STATE, :] = -(cre_t[:, :w] * pi + cim_t[:, :w] * pr)
        lr_r = jnp.minimum(lrer_ref[0, d], -1e-4)
        dt_r = jnp.exp(ldtr_ref[0, d])
        hr, hi = _cis_pow(float(t), lr_r * dt_r, limr_ref[0, d] * dt_r)
        lane_s = lax.broadcasted_iota(jnp.int32, hr.shape, 1)
        hop_ref[0, 2 * d:2 * d + 1, :] = hr
        hop_ref[0, 2 * d + 1:2 * d + 2, :] = jnp.where(lane_s < S5_STATE, -hi, hi)
    crow = lax.broadcasted_iota(jnp.int32, (S5_GROUP, 2 * w), 0)
    z = z + jnp.where((lag == 0) & (lane2 % S5_GROUP == crow), d_ref[0], 0.0)
    for s in range(t):
        off = (t - 1 - s) * S5_GROUP
        blk = z if off == 0 else pltpu.roll(z, 2 * w - off, axis=1)
        m_ref[0, s * S5_GROUP:(s + 1) * S5_GROUP, :] = _bf(blk[:, :w])


def _s5_params(lam_re, lam_im, log_dt, b_re, b_im, c_re, c_im, d_skip, t):
    g, s = S5_GROUPS, S5_STATE
    w = S5_GROUP * t
    col = lambda a: jnp.transpose(a, (1, 0, 2))[..., None]
    row = lambda a: jnp.tile(jnp.transpose(a, (1, 0, 2)), (1, 1, 2))[:, :, None, :]
    ldt = jnp.broadcast_to(log_dt[:, :, None], (2, g, s))
    tile = jnp.asarray(np.tile(np.eye(S5_GROUP, dtype=np.float32), (1, 2 * t)))
    ct = lambda a: jnp.transpose(a, (0, 2, 1))
    args = (col(lam_re), col(lam_im), col(ldt), row(lam_re), row(lam_im), row(ldt),
            b_re, b_im, ct(c_re), ct(c_im), d_skip.reshape(g, S5_GROUP, 1), tile)

    def spec(a):
        nd = a.ndim
        if a.shape[0] == g and nd > 2:
            return pl.BlockSpec((1,) + a.shape[1:], lambda i: (i,) + (0,) * (nd - 1))
        return pl.BlockSpec(a.shape, lambda i: (0,) * nd)

    return pl.pallas_call(
        functools.partial(_s5_param_kernel, t=t),
        grid=(g,),
        in_specs=[spec(a) for a in args],
        out_specs=[pl.BlockSpec((1, w, w), lambda i: (i, 0, 0)),
                   pl.BlockSpec((1, 2, 2 * s, w), lambda i: (i, 0, 0, 0)),
                   pl.BlockSpec((1, 2, 2 * s, w), lambda i: (i, 0, 0, 0)),
                   pl.BlockSpec((1, 4, 2 * s), lambda i: (i, 0, 0))],
        out_shape=[jax.ShapeDtypeStruct((g, w, w), BF16),
                   jax.ShapeDtypeStruct((g, 2, 2 * s, w), F32),
                   jax.ShapeDtypeStruct((g, 2, 2 * s, w), F32),
                   jax.ShapeDtypeStruct((g, 4, 2 * s), F32)],
        compiler_params=_params("parallel"),
        name="s5_params",
    )(*args)


def _s5_kernel(u_ref, m_ref, pt_ref, q_ref, hop_ref, y_ref, ef_ref, eb_ref, xf_ref, xb_ref, *, bsz, nck, nk):
    u = _bf(u_ref[0])
    ef_ref[...] = _dot_nt(u, _bf(pt_ref[0, 0]))
    eb_ref[...] = _dot_nt(u, _bf(pt_ref[0, 1]))
    hop = hop_ref[0]

    def step(x, e, d):
        return hop[2 * d:2 * d + 1] * x + hop[2 * d + 1:2 * d + 2] * pltpu.roll(x, S5_STATE, axis=1) + e

    def rows(kk):
        return pl.ds(pl.multiple_of(kk * bsz, bsz), bsz)

    zero = jnp.zeros((bsz, 2 * S5_STATE), F32)

    def fwd(kk, x):
        xf_ref[rows(kk), :] = x
        return step(x, ef_ref[rows(kk), :], 0)

    lax.fori_loop(0, nk, fwd, zero)

    def bwd(i, x, hi):
        kk = hi - i
        xb_ref[rows(kk), :] = x
        return step(x, eb_ref[rows(kk), :], 1)

    x = lax.fori_loop(0, nck, functools.partial(bwd, hi=nck - 1), zero)
    lax.fori_loop(0, nk - nck, functools.partial(bwd, hi=nk - 1), x)
    y_ref[0] = (_dot(u, m_ref[0]) + _dot(_bf(xf_ref[...]), _bf(q_ref[0, 0]))
                + _dot(_bf(xb_ref[...]), _bf(q_ref[0, 1])))


def _s5(p, params, ncb):
    b, nt, _ = p.shape
    nc = ncb * TOK
    n = nt - nc
    t = n // GRID_W
    w = S5_GROUP * t
    nck = nc // t
    nk = nck + GRID_W
    g = S5_GROUPS
    m, pt, q, hop = params
    cu = lax.slice_in_dim(p, COL_CU, COL_CU + WIDTH, axis=2)
    uc = cu[:, :nc].reshape(b, nck, t, g, S5_GROUP).transpose(3, 1, 0, 2, 4)
    ul = cu[:, nc:].reshape(b, t, GRID_W, g, S5_GROUP).transpose(3, 2, 0, 1, 4)
    u = jnp.concatenate([uc, ul], axis=1).reshape(g, nk * b, w)
    st = pltpu.VMEM((nk * b, 2 * S5_STATE), F32)
    y = pl.pallas_call(
        functools.partial(_s5_kernel, bsz=b, nck=nck, nk=nk),
        grid=(g,),
        in_specs=[pl.BlockSpec((1, nk * b, w), lambda i: (i, 0, 0)),
                  pl.BlockSpec((1, w, w), lambda i: (i, 0, 0)),
                  pl.BlockSpec((1, 2, 2 * S5_STATE, w), lambda i: (i, 0, 0, 0)),
                  pl.BlockSpec((1, 2, 2 * S5_STATE, w), lambda i: (i, 0, 0, 0)),
                  pl.BlockSpec((1, 4, 2 * S5_STATE), lambda i: (i, 0, 0))],
        out_specs=pl.BlockSpec((1, nk * b, w), lambda i: (i, 0, 0)),
        out_shape=jax.ShapeDtypeStruct((g, nk * b, w), F32),
        scratch_shapes=[st, st, st, st],
        compiler_params=_params("parallel"),
        name="s5",
    )(u, m, pt, q, hop)
    y = y.reshape(g, nk, b, t, S5_GROUP)
    yc = y[:, :nck].transpose(2, 1, 3, 0, 4).reshape(b, nc, WIDTH)
    yl = y[:, nck:].transpose(2, 3, 1, 0, 4).reshape(b, n, WIDTH)
    return jnp.concatenate([yc, yl], axis=1)


def _gelu_tanh(x):
    return 0.5 * x * (1.0 + jnp.tanh(np.sqrt(2.0 / np.pi).astype(np.float32) * (x + 0.044715 * (x * x * x))))


def _head_norm_gate(o, nw, gate):
    parts = []
    for h in range(HEADS):
        sl = slice(h * HEAD_DIM, (h + 1) * HEAD_DIM)
        parts.append(_rms(o[:, sl], nw))
    return jnp.concatenate(parts, axis=1) * _silu(gate)


def _merge_kernel(x_ref, mod_ref, oaf_ref, oab_ref, ga_ref, obf_ref, obb_ref, gb_ref, yc_ref,
                  g1_ref, g2_ref, g3_ref, anw_ref, bnw_ref, wglu_ref, wa_ref, wb_ref, wc_ref, wo_ref,
                  o_ref, *, d):
    ya = _dot(_bf(_head_norm_gate(oaf_ref[0] + oab_ref[0], anw_ref[...], ga_ref[0])), wa_ref[...])
    yb = _dot(_bf(_head_norm_gate(obf_ref[0] + obb_ref[0], bnw_ref[...], gb_ref[0])), wb_ref[...])
    z = _gelu_tanh(yc_ref[0])
    ycc = _dot(_bf(z * _sigmoid(_dot(_bf(z), wglu_ref[...]))), wc_ref[...])
    mix = _sigmoid(g1_ref[0]) * ya + _sigmoid(g2_ref[0]) * yb + _sigmoid(g3_ref[0]) * ycc
    o_ref[0] = x_ref[0] + mod_ref[0, :, 2 * d:3 * d] * _dot(_bf(mix), wo_ref[...])


def _mod_spec(d, ncb, off):
    return pl.BlockSpec((1, 1, 6 * d), lambda i, j: (2 * i + (j + off >= ncb).astype(jnp.int32), 0, 0))


def _merge(xs, mod, p, oa, ob, yc, anw, bnw, wglu, wa, wb, wc, wo, ncb):
    b, nt, d = xs.shape

    def tok(w, col=0):
        return pl.BlockSpec((1, TOK, w), lambda i, j: (i, j, col))

    def full(a):
        nd = a.ndim
        return pl.BlockSpec(a.shape, lambda i, j: (0,) * nd)

    cm = COL_MERGE // d
    weights = (anw, bnw, wglu, wa, wb, wc, wo)
    return pl.pallas_call(
        functools.partial(_merge_kernel, d=d),
        grid=(b, nt // TOK),
        in_specs=[tok(d), _mod_spec(d, ncb, 0),
                  tok(WIDTH), tok(WIDTH), tok(WIDTH, COL_A // WIDTH + 4),
                  tok(WIDTH), tok(WIDTH), tok(WIDTH, COL_GATE_B // WIDTH), tok(WIDTH),
                  tok(d, cm), tok(d, cm + 1), tok(d, cm + 2)] + [full(a) for a in weights],
        out_specs=tok(d),
        out_shape=jax.ShapeDtypeStruct((b, nt, d), F32),
        compiler_params=_params("parallel", "parallel"),
        name="merge",
    )(xs, mod, oa[0], oa[1], p, ob[0], ob[1], p, yc, p, p, p, *weights)


def _mlp_kernel(x_ref, mod_ref, nw_ref, w1_ref, w2_ref, fw_ref, o_ref, *, d, dff, final):
    x = x_ref[0]
    h = _bf(_rms(x, nw_ref[...]) * (1.0 + mod_ref[0, :, 4 * d:5 * d]) + mod_ref[0, :, 3 * d:4 * d])
    acc = jnp.zeros_like(x)
    for c in range(dff // d):
        a = jnp.maximum(_dot(h, w1_ref[:, c * d:(c + 1) * d]), 0.0)
        acc = acc + _dot(_bf(a * a), w2_ref[c * d:(c + 1) * d, :])
    y = x + mod_ref[0, :, 5 * d:6 * d] * acc
    if final:
        y = _rms(y, fw_ref[...])
    o_ref[0] = y


def _mlp(xs, mod, nw, w1, w2, fw, ncb, final):
    b, nt, d = xs.shape
    dff = w1.shape[1]
    off = ncb if final else 0
    nblk = nt // TOK - off

    def full(a):
        nd = a.ndim
        return pl.BlockSpec(a.shape, lambda i, j: (0,) * nd)

    return pl.pallas_call(
        functools.partial(_mlp_kernel, d=d, dff=dff, final=final),
        grid=(b, nblk),
        in_specs=[pl.BlockSpec((1, TOK, d), lambda i, j: (i, j + off, 0)), _mod_spec(d, ncb, off),
                  full(nw), full(w1), full(w2), full(fw)],
        out_specs=pl.BlockSpec((1, TOK, d), lambda i, j: (i, j, 0)),
        out_shape=jax.ShapeDtypeStruct((b, nblk * TOK, d), F32),
        compiler_params=_params("parallel", "parallel"),
        name="mlp_final" if final else "mlp",
    )(xs, mod, nw, w1, w2, fw)


def kernel(x, c, ctx, c_ctx, ada_w, ada_b, norm1_w, w_in, hgrn_lb_logits, hgrn_norm_w, gdn_conv_w, gdn_a_log, gdn_dt_bias, gdn_norm_w, s5_lam_re, s5_lam_im, s5_log_dt, s5_b_re, s5_b_im, s5_c_re, s5_c_im, s5_d, s5_w_glu, w_branch_a, w_branch_b, w_branch_c, w_out, norm2_w, w_ff1, w_ff2, final_norm_w):
    b, n, d = x.shape
    nc = ctx.shape[1]
    depth = ada_w.shape[0]
    assert n % (GRID_W * SUBLANES) == 0 and n % TOK == 0 and nc % TOK == 0
    t = n // GRID_W
    assert nc % t == 0 and d % WIDTH == 0 and COL_MERGE % d == 0
    ncb = nc // TOK

    r = -(-(b + 1) // SUBLANES) * SUBLANES
    c_rows = jnp.concatenate([c, c_ctx[None], jnp.zeros((r - b - 1, d), F32)], axis=0)
    mods = _ada(c_rows, ada_w, ada_b)

    xs = jnp.concatenate([ctx, x], axis=1)
    for l in range(depth):
        last = l == depth - 1
        mod = jnp.stack([jnp.broadcast_to(mods[l, b], (b, 6 * d)), mods[l, :b]], axis=1).reshape(2 * b, 1, 6 * d)
        wl = w_in[l]
        s = np.cumsum([0, WIDTH, WIDTH, 2 * WIDTH, WIDTH, 3 * WIDTH, WIDTH, 2 * HEADS, 2 * HEADS, WIDTH, 3 * d])
        w_perm = jnp.concatenate(
            [wl[:, s[4]:s[5]], wl[:, s[0]:s[4]], wl[:, s[5]:s[6]], wl[:, s[8]:s[9]], wl[:, s[9]:s[10]],
             wl[:, s[6]:s[8]], jnp.zeros((d, LANES - 4 * HEADS), F32)], axis=1)
        p = _inproj(xs, mod, norm1_w[l].reshape(1, d), _bf(w_perm), ncb)
        oa = _hgrn(p, hgrn_lb_logits, l, ncb)
        qkv, bg = _gdn_prep(p, gdn_conv_w[l], gdn_a_log[l], gdn_dt_bias[l], ncb)
        ob = _gdn(qkv, bg, ncb)
        s5p = _s5_params(s5_lam_re[l], s5_lam_im[l], s5_log_dt[l], s5_b_re[l], s5_b_im[l],
                         s5_c_re[l], s5_c_im[l], s5_d[l], t)
        yc = _s5(p, s5p, ncb)
        xs = _merge(xs, mod, p, oa, ob, yc, hgrn_norm_w[l].reshape(1, -1), gdn_norm_w[l].reshape(1, -1),
                    _bf(s5_w_glu[l]), _bf(w_branch_a[l]), _bf(w_branch_b[l]), _bf(w_branch_c[l]),
                    _bf(w_out[l]), ncb)
        xs = _mlp(xs, mod, norm2_w[l].reshape(1, d), _bf(w_ff1[l]), _bf(w_ff2[l]),
                  final_norm_w.reshape(1, d), ncb, last)
    return xs
```
